```python
import jax, jax.numpy as jnp
from jax import lax
import numpy as np

D_MODEL = 1024
BATCH = 4
SEQ = 4096
DEPTH = 2

N_MIXERS = 2

A_HEADS = 8
A_HEAD_DIM = 128
A_WIDTH = A_HEADS * A_HEAD_DIM
A_KV_LATENT = 256
IDX_HEADS = 8
IDX_DIM = 64
TOPK_MAX = 256
Q_BLOCK = 128
A_IN_COLS = A_WIDTH + A_KV_LATENT + IDX_HEADS * IDX_DIM + IDX_DIM + IDX_HEADS + A_WIDTH

B_HEADS = 4
B_WIDTH = 2 * D_MODEL
B_V_DIM = B_WIDTH // B_HEADS
B_QK_DIM = B_V_DIM // 2
B_QK_COLS = 2 * B_HEADS * B_QK_DIM
CONV_WIDTH = 4
CHUNK = 64
B_IN_COLS = B_QK_COLS + B_WIDTH + 2 * B_HEADS + B_WIDTH + B_WIDTH

N_A_LAYERS = (DEPTH + 1) // 2
N_B_LAYERS = DEPTH // 2
DEEPNORM_ALPHA = (2 * DEPTH) ** 0.25
DEEPNORM_BETA = (8 * DEPTH) ** -0.25
NORM_EPS = 1e-5

kernel_name = "dsa_mlstm_interleaved_deepnorm"


def _split_points(sizes):
    return [int(s) for s in np.cumsum(sizes)]


def _layernorm(x, g, b):
    xf = x.astype(jnp.float32)
    mu = jnp.mean(xf, axis=-1, keepdims=True)
    var = jnp.mean(jnp.square(xf - mu), axis=-1, keepdims=True)
    return ((xf - mu) * lax.rsqrt(var + NORM_EPS) * g + b).astype(x.dtype)


def _rmsnorm(x, g):
    xf = x.astype(jnp.float32)
    return xf * lax.rsqrt(jnp.mean(xf * xf, axis=-1, keepdims=True) + NORM_EPS) * g


def _dsa_mixer(x, w_in, kv_norm_g, w_uk, w_uv, w_out):
    f32 = jnp.float32
    B, T, _ = x.shape
    topk = min(TOPK_MAX, T // 4)
    nb = T // Q_BLOCK
    proj = x @ w_in
    q, c_kv, q_idx, k_idx, w_idx, z = jnp.split(
        proj, _split_points([A_WIDTH, A_KV_LATENT, IDX_HEADS * IDX_DIM, IDX_DIM, IDX_HEADS]), axis=-1)
    q = q.reshape(B, T, A_HEADS, A_HEAD_DIM)
    c_kv = _rmsnorm(c_kv, kv_norm_g)
    q_lat = jnp.einsum('bthd,hdc->bthc', q, w_uk).astype(f32) * (A_HEAD_DIM ** -0.5)
    q_idx = q_idx.reshape(B, T, IDX_HEADS, IDX_DIM).astype(f32)
    k_idx = k_idx.astype(f32)
    w_idx = w_idx.astype(f32)
    slopes = 2.0 ** (-8.0 * jnp.arange(1, A_HEADS + 1, dtype=f32) / A_HEADS)
    s_pos = jnp.arange(T)

    def to_blocks(a):
        return a.reshape(B, nb, Q_BLOCK, *a.shape[2:]).swapaxes(0, 1)

    def block(args):
        blk, qb, qib, wb = args
        t_pos = blk * Q_BLOCK + jnp.arange(Q_BLOCK)
        rel = jax.nn.relu(jnp.einsum('bqhd,bsd->bqhs', qib, k_idx))
        idx_score = jnp.einsum('bqh,bqhs->bqs', wb, rel)
        causal = s_pos[None, :] <= t_pos[:, None]
        idx_score = jnp.where(causal[None], idx_score, -jnp.inf)
        _, sel = lax.top_k(idx_score, topk)
        c_sel = jax.vmap(lambda c, i: c[i])(c_kv, sel)
        logits = jnp.einsum('bqhc,bqkc->bqhk', qb, c_sel)
        dist = t_pos[None, :, None] - sel
        logits = logits - slopes[None, None, :, None] * dist[:, :, None, :].astype(f32)
        logits = jnp.where((dist >= 0)[:, :, None, :], logits, -jnp.inf)
        p = jax.nn.softmax(logits, axis=-1)
        return jnp.einsum('bqhk,bqkc->bqhc', p, c_sel)

    o_lat = lax.map(block, (jnp.arange(nb), to_blocks(q_lat), to_blocks(q_idx), to_blocks(w_idx)))
    o_lat = o_lat.swapaxes(0, 1).reshape(B, T, A_HEADS, A_KV_LATENT)
    o = jnp.einsum('bthc,hcd->bthd', o_lat, w_uv.astype(f32)).reshape(B, T, A_WIDTH).astype(x.dtype)
    return (o * jax.nn.silu(z)) @ w_out


def _causal_conv(u, w, b):
    T = u.shape[1]
    up = jnp.pad(u, ((0, 0), (CONV_WIDTH - 1, 0), (0, 0)))
    return sum(up[:, j:j + T] * w[j] for j in range(CONV_WIDTH)) + b


def _mlstm_mixer(x, w_in, i_bias, f_bias, conv_w, conv_b, head_norm_g, w_out):
    f32 = jnp.float32
    B, T, _ = x.shape
    nc = T // CHUNK
    proj = x @ w_in
    qk, v, ig, fg, o, z = jnp.split(
        proj, _split_points([B_QK_COLS, B_WIDTH, B_HEADS, B_HEADS, B_WIDTH]), axis=-1)
    qk = jax.nn.silu(_causal_conv(qk, conv_w, conv_b))
    q, k = jnp.split(qk, 2, axis=-1)

    def heads(a, d):
        return a.reshape(B, nc, CHUNK, B_HEADS, d).transpose(1, 0, 3, 2, 4).astype(f32)

    def gates(a):
        return a.astype(f32).reshape(B, nc, CHUNK, B_HEADS).transpose(1, 0, 3, 2)

    q = heads(q, B_QK_DIM) * (B_QK_DIM ** -0.5)
    k = heads(k, B_QK_DIM)
    v = heads(v, B_V_DIM)
    log_i = gates(ig + i_bias)
    log_f = jax.nn.log_sigmoid(gates(fg + f_bias))
    causal = jnp.tril(jnp.ones((CHUNK, CHUNK), dtype=bool))

    def step(carry, inp):
        C, n, m = carry
        qc, kc, vc, ic, fc = inp
        b = jnp.cumsum(fc, axis=-1)
        d_log = jnp.where(causal, b[..., :, None] - b[..., None, :] + ic[..., None, :], -jnp.inf)
        m_inter = b + m[..., None]
        m_t = jnp.maximum(m_inter, jnp.max(d_log, axis=-1))
        s = jnp.einsum('bhtd,bhsd->bhts', qc, kc) * jnp.exp(d_log - m_t[..., None])
        g_inter = jnp.exp(m_inter - m_t)
        num = jnp.einsum('bhts,bhsv->bhtv', s, vc) + g_inter[..., None] * jnp.einsum('bhtd,bhdv->bhtv', qc, C)
        den = jnp.sum(s, axis=-1) + g_inter * jnp.einsum('bhtd,bhd->bht', qc, n)
        h = num / jnp.maximum(jnp.abs(den), jnp.exp(-m_t))[..., None]
        b_end = b[..., -1]
        w_log = b_end[..., None] - b + ic
        m_new = jnp.maximum(b_end + m, jnp.max(w_log, axis=-1))
        decay = jnp.exp(b_end + m - m_new)
        w_s = jnp.exp(w_log - m_new[..., None])
        C_new = decay[..., None, None] * C + jnp.einsum('bhs,bhsd,bhsv->bhdv', w_s, kc, vc)
        n_new = decay[..., None] * n + jnp.einsum('bhs,bhsd->bhd', w_s, kc)
        return (C_new, n_new, m_new), h

    init = (jnp.zeros((B, B_HEADS, B_QK_DIM, B_V_DIM), f32),
            jnp.zeros((B, B_HEADS, B_QK_DIM), f32),
            jnp.zeros((B, B_HEADS), f32))
    _, h = lax.scan(step, init, (q, k, v, log_i, log_f))
    h = h.transpose(1, 0, 3, 2, 4).reshape(B, T, B_HEADS, B_V_DIM)
    h = _rmsnorm(h, head_norm_g)
    h = jax.nn.sigmoid(o.astype(f32)).reshape(B, T, B_HEADS, B_V_DIM) * h
    h = h.reshape(B, T, B_WIDTH).astype(x.dtype)
    return (h * jax.nn.silu(z)) @ w_out


def setup_inputs(seed: int = 0) -> dict:
    key = jax.random.key(seed)
    ks = jax.random.split(key, 20)
    f32 = jnp.float32

    def nrm(k, shape, scale):
        return jax.random.normal(k, shape, f32) * scale

    nA, nB = N_A_LAYERS, N_B_LAYERS
    return {
        "x": nrm(ks[0], (BATCH, SEQ, D_MODEL), 1.0),
        "a_w_in": nrm(ks[1], (nA, D_MODEL, A_IN_COLS), D_MODEL ** -0.5),
        "a_kv_norm_g": 1.0 + nrm(ks[2], (nA, A_KV_LATENT), 0.02),
        "a_w_uk": nrm(ks[3], (nA, A_HEADS, A_HEAD_DIM, A_KV_LATENT), A_KV_LATENT ** -0.5),
        "a_w_uv": nrm(ks[4], (nA, A_HEADS, A_KV_LATENT, A_HEAD_DIM), A_KV_LATENT ** -0.5),
        "a_w_out": nrm(ks[5], (nA, A_WIDTH, D_MODEL), A_WIDTH ** -0.5 * DEEPNORM_BETA),
        "a_ln_g": 1.0 + nrm(ks[6], (nA, D_MODEL), 0.02),
        "a_ln_b": nrm(ks[7], (nA, D_MODEL), 0.02),
        "b_w_in": nrm(ks[8], (nB, D_MODEL, B_IN_COLS), D_MODEL ** -0.5),
        "b_i_bias": nrm(ks[9], (nB, B_HEADS), 0.1),
        "b_f_bias": jnp.linspace(3.0, 6.0, B_HEADS, dtype=f32)[None, :] + nrm(ks[10], (nB, B_HEADS), 0.1),
        "b_conv_w": nrm(ks[11], (nB, CONV_WIDTH, B_QK_COLS), CONV_WIDTH ** -0.5),
        "b_conv_b": nrm(ks[12], (nB, B_QK_COLS), 0.02),
        "b_head_norm_g": 1.0 + nrm(ks[13], (nB, B_HEADS, B_V_DIM), 0.02),
        "b_w_out": nrm(ks[14], (nB, B_WIDTH, D_MODEL), B_WIDTH ** -0.5 * DEEPNORM_BETA),
        "b_ln_g": 1.0 + nrm(ks[15], (nB, D_MODEL), 0.02),
        "b_ln_b": nrm(ks[16], (nB, D_MODEL), 0.02),
    }


def reference(x, a_w_in, a_kv_norm_g, a_w_uk, a_w_uv, a_w_out, a_ln_g, a_ln_b,
              b_w_in, b_i_bias, b_f_bias, b_conv_w, b_conv_b, b_head_norm_g, b_w_out,
              b_ln_g, b_ln_b):
    for layer in range(DEPTH):
        j = layer // N_MIXERS
        if layer % N_MIXERS == 0:
            y = _dsa_mixer(x, a_w_in[j], a_kv_norm_g[j], a_w_uk[j], a_w_uv[j], a_w_out[j])
            x = _layernorm(DEEPNORM_ALPHA * x + y, a_ln_g[j], a_ln_b[j])
        else:
            y = _mlstm_mixer(x, b_w_in[j], b_i_bias[j], b_f_bias[j], b_conv_w[j], b_conv_b[j],
                             b_head_norm_g[j], b_w_out[j])
            x = _layernorm(DEEPNORM_ALPHA * x + y, b_ln_g[j], b_ln_b[j])
    return x
```

```python
import functools

import jax
import jax.numpy as jnp
from jax import lax
from jax.experimental import pallas as pl
from jax.experimental.pallas import tpu as pltpu

_F32 = jnp.float32
_MXU = jnp.bfloat16

A_HEADS = 8
A_HEAD_DIM = 128
A_KV_LATENT = 256
IDX_HEADS = 8
IDX_DIM = 64
TOPK_MAX = 256
IDX_COLS = 640
B_HEADS = 4
B_V_DIM = 512
B_QK_DIM = 256
CONV_WIDTH = 4
DEPTH = 2
DEEPNORM_ALPHA = (2 * DEPTH) ** 0.25
NORM_EPS = 1e-5

LANES = 128
SUBLANES = 8
VMEM_LIMIT = 56 * 1024 * 1024

_NEG_INF_KEY = -(2 ** 31) + 0x7FFFFF
_INT_MIN = -(2 ** 31)


def _dot(a, b):
    return jnp.dot(a, b, preferred_element_type=_F32)


def _dot_nt(a, b):
    return lax.dot_general(a, b, (((1,), (1,)), ((), ())), preferred_element_type=_F32)


def _dot_tn(a, b):
    return lax.dot_general(a, b, (((0,), (0,)), ((), ())), preferred_element_type=_F32)


def _split(x):
    hi = x.astype(_MXU)
    lo = (x - hi.astype(_F32)).astype(_MXU)
    return hi, lo


def _split3(x):
    v1 = x.astype(_MXU)
    r1 = x - v1.astype(_F32)
    v2 = r1.astype(_MXU)
    v3 = (r1 - v2.astype(_F32)).astype(_MXU)
    return v1, v2, v3


def _sigmoid(x):
    return 1.0 / (1.0 + jnp.exp(-x))


def _layernorm(r, g, b):
    mu = jnp.mean(r, axis=-1, keepdims=True)
    d = r - mu
    var = jnp.mean(d * d, axis=-1, keepdims=True)
    return d * lax.rsqrt(var + NORM_EPS) * g + b


def _proj_a_body(x_ref, wq_ref, wz_ref, wc_ref, wih_ref, wil_ref, g_ref,
                 q_ref, z_ref, ckv_ref, idx_ref):
    x = x_ref[...]
    xh, xl = _split(x)
    q_ref[...] = _dot(xh, wq_ref[...]).astype(q_ref.dtype)
    z_ref[...] = _dot(xh, wz_ref[...])
    c = _dot(xh, wc_ref[...])
    c = c * lax.rsqrt(jnp.mean(c * c, axis=-1, keepdims=True) + NORM_EPS) * g_ref[...]
    ckv_ref[...] = c.astype(ckv_ref.dtype)
    wih = wih_ref[...]
    idx_ref[...] = _dot(xh, wih) + _dot(xl, wih) + _dot(xh, wil_ref[...])


def _proj_a(x2, wq, wz, wc, wih, wil, g, *, tm):
    M, D = x2.shape
    full = lambda a: pl.BlockSpec(a.shape, lambda i: (0,) * a.ndim)
    row = lambda n: pl.BlockSpec((tm, n), lambda i: (i, 0))
    return pl.pallas_call(
        _proj_a_body,
        grid=(M // tm,),
        in_specs=[row(D), full(wq), full(wz), full(wc), full(wih), full(wil), full(g)],
        out_specs=[row(wq.shape[1]), row(wz.shape[1]), row(wc.shape[1]), row(IDX_COLS)],
        out_shape=[jax.ShapeDtypeStruct((M, wq.shape[1]), _MXU),
                   jax.ShapeDtypeStruct((M, wz.shape[1]), _F32),
                   jax.ShapeDtypeStruct((M, wc.shape[1]), _MXU),
                   jax.ShapeDtypeStruct((M, IDX_COLS), _F32)],
        compiler_params=pltpu.CompilerParams(
            dimension_semantics=("arbitrary",), vmem_limit_bytes=VMEM_LIMIT),
        name="dsa_proj",
    )(x2, wq, wz, wc, wih, wil, g)


def _attn_body(q_ref, idxq_ref, kidx_ref, ckv_ref, z_ref, x_ref, wuk_ref, wuv_ref, wout_ref,
               g_ref, b_ref, o_ref,
               s_ref, k3_ref, qi3_ref, qlat_ref, wb_ref, p_ref, m_ref, l_ref, acc_ref,
               tau_ref, pj_ref, og_ref, *, T, TQ, topk, idx_bits):
    H = A_HEADS
    TK = TQ
    i = pl.program_id(1)
    nk = i + 1
    lane = lax.broadcasted_iota(jnp.int32, (TQ, LANES), 1)
    low = lane < IDX_DIM

    @pl.when(i == 0)
    def _():
        kk = kidx_ref[...]
        lane_t = lax.broadcasted_iota(jnp.int32, kk.shape, 1)
        hi = kk.astype(_MXU).astype(_F32)
        lo = kk - hi
        k3_ref[:, 0:LANES] = jnp.where(lane_t < IDX_DIM, hi,
                                       pltpu.roll(hi, IDX_DIM, axis=1)).astype(_MXU)
        k3_ref[:, LANES:2 * LANES] = jnp.where(lane_t < IDX_DIM, lo, 0.0).astype(_MXU)

    scale = A_HEAD_DIM ** -0.5
    for h in range(H):
        rows = slice(h * TQ, (h + 1) * TQ)
        ch = idxq_ref[:, LANES * (h // 2):LANES * (h // 2 + 1)]
        if h % 2:
            ch = pltpu.roll(ch, IDX_DIM, axis=1)
        hi = ch.astype(_MXU).astype(_F32)
        lo = ch - hi
        qi3_ref[rows, 0:LANES] = jnp.where(low, hi, pltpu.roll(lo, IDX_DIM, axis=1)).astype(_MXU)
        qi3_ref[rows, LANES:2 * LANES] = jnp.where(low, hi, 0.0).astype(_MXU)
        wcol = idxq_ref[:, IDX_HEADS * IDX_DIM + IDX_DIM + h:IDX_HEADS * IDX_DIM + IDX_DIM + h + 1]
        wb_ref[rows, :] = jnp.broadcast_to(wcol, (TQ, TK))
        ql = _dot(q_ref[:, A_HEAD_DIM * h:A_HEAD_DIM * (h + 1)], wuk_ref[h]) * scale
        qlat_ref[rows, :] = ql.astype(_MXU)

    t_pos = i * TQ + lax.broadcasted_iota(jnp.int32, (TQ, TK), 0)
    col = lax.broadcasted_iota(jnp.int32, (TQ, TK), 1)

    def score_tile(j, carry):
        off = pl.multiple_of(j * TK, TK)
        rel = _dot_nt(qi3_ref[...], k3_ref[pl.ds(off, TK), :])
        prod = jnp.maximum(rel, 0.0) * wb_ref[...]
        sc = prod[0:TQ]
        for h in range(1, H):
            sc = sc + prod[h * TQ:(h + 1) * TQ]
        sc = jnp.where(col + off <= t_pos, sc, -jnp.inf)
        bits = lax.bitcast_convert_type(sc, jnp.int32)
        bits = jnp.where(sc == 0.0, 0, bits)
        s_ref[:, pl.ds(off, TK)] = bits ^ ((bits >> 31) & 0x7FFFFFFF)
        return carry

    lax.fori_loop(0, nk, score_tile, 0)

    def count(pred):
        def body(j, cnt):
            off = pl.multiple_of(j * TK, TK)
            for c in range(TK // LANES):
                o = off + c * LANES
                cnt = cnt + pred(s_ref[:, pl.ds(o, LANES)], o).astype(jnp.int32)
            return cnt
        cnt = lax.fori_loop(0, nk, body, jnp.zeros((TQ, LANES), jnp.int32))
        return jnp.sum(cnt, axis=1, keepdims=True)

    def bisect_key(it, tau):
        cand = tau + jnp.left_shift(jnp.int32(1), 31 - it)
        candb = jnp.broadcast_to(cand, (TQ, LANES))
        n = count(lambda blk, o: blk >= candb)
        return jnp.where(n >= topk, cand, tau)

    tau = lax.fori_loop(0, 32, bisect_key, jnp.full((TQ, 1), _INT_MIN, jnp.int32))
    taub = jnp.broadcast_to(tau, (TQ, LANES))
    n_gt = count(lambda blk, o: blk > taub)
    n_eq = count(lambda blk, o: blk == taub)
    need = topk - n_gt
    tau_ref[...] = taub
    pj_ref[...] = jnp.broadcast_to(jnp.where(tau == _NEG_INF_KEY, -1, T), (TQ, LANES))

    @pl.when(jnp.max(jnp.where(tau == _NEG_INF_KEY, 0, n_eq - need)) > 0)
    def _():
        def bisect_idx(it, pj):
            cand = pj + jnp.left_shift(jnp.int32(1), idx_bits - 1 - it)
            candb = jnp.broadcast_to(cand, (TQ, LANES))
            n = count(lambda blk, o: (blk == taub) & (lane + o < candb))
            return jnp.where(n < need, cand, pj)
        pj = lax.fori_loop(0, idx_bits, bisect_idx, jnp.zeros((TQ, 1), jnp.int32))
        pj_ref[...] = jnp.broadcast_to(jnp.where(tau == _NEG_INF_KEY, -1, pj), (TQ, LANES))

    m_ref[...] = jnp.full(m_ref.shape, -1e30, _F32)
    l_ref[...] = jnp.zeros(l_ref.shape, _F32)
    acc_ref[...] = jnp.zeros(acc_ref.shape, _F32)
    reps = TK // LANES
    tau_t = jnp.concatenate([tau_ref[...]] * reps, axis=1)
    pj_t = jnp.concatenate([pj_ref[...]] * reps, axis=1)

    def attend_tile(j, carry):
        off = pl.multiple_of(j * TK, TK)
        keyb = s_ref[:, pl.ds(off, TK)]
        sidx = col + off
        sel = (keyb > tau_t) | ((keyb == tau_t) & (sidx <= pj_t))
        dist = (t_pos - sidx).astype(_F32)
        ct = ckv_ref[pl.ds(off, TK), :]
        lg_all = _dot_nt(qlat_ref[...], ct)
        for h in range(H):
            rows = slice(h * TQ, (h + 1) * TQ)
            slope = 2.0 ** (-8.0 * (h + 1) / A_HEADS)
            lg = jnp.where(sel, lg_all[rows] - slope * dist, -jnp.inf)
            m_prev = m_ref[rows, :]
            m_new = jnp.maximum(m_prev, jnp.max(lg, axis=1, keepdims=True))
            alpha = jnp.exp(m_prev - m_new)
            p = jnp.exp(lg - m_new)
            l_ref[rows, :] = alpha * l_ref[rows, :] + jnp.sum(p, axis=1, keepdims=True)
            m_ref[rows, :] = m_new
            p_ref[rows, :] = p.astype(_MXU)
            acc_ref[rows, :] = acc_ref[rows, :] * alpha
        acc_ref[...] += _dot(p_ref[...], ct)
        return carry

    lax.fori_loop(0, nk, attend_tile, 0)

    for h in range(H):
        rows = slice(h * TQ, (h + 1) * TQ)
        ol = acc_ref[rows, :] / l_ref[rows, :]
        og_ref[:, A_HEAD_DIM * h:A_HEAD_DIM * (h + 1)] = _dot(ol.astype(_MXU), wuv_ref[h])
    z = z_ref[...]
    gated = og_ref[...] * (z * _sigmoid(z))
    y = _dot(gated.astype(_MXU), wout_ref[...])
    o_ref[...] = _layernorm(DEEPNORM_ALPHA * x_ref[...] + y, g_ref[...], b_ref[...])


def _attention(q, idx, ckv, z, x2, wuk, wuv, wout, g, b, *, B, T, TQ):
    D = x2.shape[1]
    NQ = T // TQ
    topk = min(TOPK_MAX, T // 4)
    assert TQ >= topk and T % TQ == 0 and TQ % LANES == 0
    idx_bits = max(1, (T - 1).bit_length())
    H = A_HEADS
    full = lambda a: pl.BlockSpec(a.shape, lambda bb, i: (0,) * a.ndim)
    qrow = lambda n: pl.BlockSpec((TQ, n), lambda bb, i: (bb * NQ + i, 0))
    body = functools.partial(_attn_body, T=T, TQ=TQ, topk=topk, idx_bits=idx_bits)
    return pl.pallas_call(
        body,
        grid=(B, NQ),
        in_specs=[qrow(q.shape[1]), qrow(IDX_COLS),
                  pl.BlockSpec((T, LANES), lambda bb, i: (bb, IDX_HEADS * IDX_DIM // LANES)),
                  pl.BlockSpec((T, A_KV_LATENT), lambda bb, i: (bb, 0)),
                  qrow(z.shape[1]), qrow(D), full(wuk), full(wuv), full(wout), full(g), full(b)],
        out_specs=qrow(D),
        out_shape=jax.ShapeDtypeStruct(x2.shape, _F32),
        scratch_shapes=[
            pltpu.VMEM((TQ, T), jnp.int32),
            pltpu.VMEM((T, 2 * LANES), _MXU),
            pltpu.VMEM((H * TQ, 2 * LANES), _MXU),
            pltpu.VMEM((H * TQ, A_KV_LATENT), _MXU),
            pltpu.VMEM((H * TQ, TQ), _F32),
            pltpu.VMEM((H * TQ, TQ), _MXU),
            pltpu.VMEM((H * TQ, 1), _F32),
            pltpu.VMEM((H * TQ, 1), _F32),
            pltpu.VMEM((H * TQ, A_KV_LATENT), _F32),
            pltpu.VMEM((TQ, LANES), jnp.int32),
            pltpu.VMEM((TQ, LANES), jnp.int32),
            pltpu.VMEM((TQ, H * A_HEAD_DIM), _F32),
        ],
        compiler_params=pltpu.CompilerParams(
            dimension_semantics=("arbitrary", "arbitrary"), vmem_limit_bytes=VMEM_LIMIT),
        name="dsa_attention",
    )(q, idx, idx, ckv, z, x2, wuk, wuv, wout, g, b)


def _log_sigmoid(x):
    return -(jnp.maximum(-x, 0.0) + jnp.log1p(jnp.exp(-jnp.abs(x))))


def _mlstm_body(x_ref, wh_ref, wg_ref, wgt_ref, gb_ref, cw_ref, cb_ref, hng_ref, wout_ref,
                g_ref, b_ref, o_ref,
                x3_ref, cst_ref, mst_ref, tail_ref, yacc_ref, *, L):
    D = x_ref.shape[1]
    DK, DV = B_QK_DIM, B_V_DIM
    c = pl.program_id(1)
    h = pl.program_id(2)

    @pl.when(h == 0)
    def _():
        hi, lo = _split(x_ref[...])
        x3_ref[:, 0:D] = hi
        x3_ref[:, D:2 * D] = lo
        x3_ref[:, 2 * D:3 * D] = hi
        yacc_ref[...] = jnp.zeros(yacc_ref.shape, _F32)

    @pl.when(c == 0)
    def _():
        cst_ref[h] = jnp.zeros(cst_ref.shape[1:], _F32)
        mst_ref[h] = jnp.zeros(mst_ref.shape[1:], _F32)
        tail_ref[h] = jnp.zeros(tail_ref.shape[1:], _F32)

    proj = _dot(x3_ref[:, 0:D], wh_ref[h])
    u = proj[:, 0:2 * DK]
    v = proj[:, 2 * DK:2 * DK + DV]
    og = proj[:, 2 * DK + DV:2 * DK + 2 * DV]
    z = proj[:, 2 * DK + 2 * DV:2 * DK + 3 * DV]
    gcol = _dot(x3_ref[...], wg_ref[h])
    grow = _dot_nt(wgt_ref[h], x3_ref[...])
    gb = gb_ref[h]

    cw = cw_ref[h]
    tail = tail_ref[h]
    row8 = lax.broadcasted_iota(jnp.int32, (SUBLANES, 2 * DK), 0)
    conv = u * cw[CONV_WIDTH - 1:CONV_WIDTH, :] + cb_ref[h]
    for d in range(1, CONV_WIDTH):
        sh = pltpu.roll(u, d, axis=0)
        fix = jnp.where(row8 < d, pltpu.roll(tail, d, axis=0), sh[0:SUBLANES])
        sh = jnp.concatenate([fix, sh[SUBLANES:]], axis=0)
        conv = conv + sh * cw[CONV_WIDTH - 1 - d:CONV_WIDTH - d, :]
    tail_ref[h] = u[L - SUBLANES:L]
    qk = conv * _sigmoid(conv)
    qb = (qk[:, 0:DK] * (DK ** -0.5)).astype(_MXU)
    kb = qk[:, DK:2 * DK].astype(_MXU)

    i_col = gcol[:, 0:1] + gb[0:1, 0:1]
    lf_col = _log_sigmoid(gcol[:, 1:2] + gb[1:2, 0:1])
    i_row = grow[0:1, :] + gb[0:1, 0:1]
    lf_row = _log_sigmoid(grow[1:2, :] + gb[1:2, 0:1])

    r_i = lax.broadcasted_iota(jnp.int32, (L, L), 0)
    c_i = lax.broadcasted_iota(jnp.int32, (L, L), 1)
    causal = c_i <= r_i
    tril = causal.astype(_MXU)
    triu = (r_i <= c_i).astype(_MXU)
    c1, c2, c3 = _split3(jnp.broadcast_to(lf_col, (L, LANES)))
    b_col = (_dot(tril, c1) + _dot(tril, c2) + _dot(tril, c3))[:, 0:1]
    r1, r2, r3 = _split3(jnp.broadcast_to(lf_row, (SUBLANES, L)))
    b_row = (_dot(r1, triu) + _dot(r2, triu) + _dot(r3, triu))[0:1, :]

    m_prev = mst_ref[h][0:1, 0:1]
    dlog = jnp.where(causal, b_col + (i_row - b_row), -jnp.inf)
    m_t = jnp.maximum(b_col + m_prev, jnp.max(dlog, axis=1, keepdims=True))
    s = _dot_nt(qb, kb) * jnp.exp(dlog - m_t)
    g_inter = jnp.exp(b_col + m_prev - m_t)
    vext = jnp.concatenate([v, jnp.ones((L, LANES), _F32)], axis=1)
    cprev = cst_ref[h]
    numden = _dot(s.astype(_MXU), vext.astype(_MXU)) + g_inter * _dot(qb, cprev.astype(_MXU))
    den = jnp.maximum(jnp.abs(numden[:, DV:DV + LANES]), jnp.exp(-m_t))
    hh = numden[:, 0:DV] / jnp.concatenate([den] * (DV // LANES), axis=1)

    b_end = b_col[L - 1:L, :]
    w_log = b_end - b_col + i_col
    m_new = jnp.maximum(b_end + m_prev, jnp.max(w_log, axis=0, keepdims=True))
    decay = jnp.exp(b_end + m_prev - m_new)
    w_s = jnp.exp(w_log - m_new)
    cst_ref[h] = decay * cprev + _dot_tn(kb, (w_s * vext).astype(_MXU))
    mst_ref[h] = jnp.broadcast_to(m_new, mst_ref.shape[1:])

    hn = hh * lax.rsqrt(jnp.mean(hh * hh, axis=1, keepdims=True) + NORM_EPS) * hng_ref[h]
    hg = _sigmoid(og) * hn * (z * _sigmoid(z))
    yacc_ref[...] += _dot(hg.astype(_MXU), wout_ref[h])

    @pl.when(h == B_HEADS - 1)
    def _():
        o_ref[...] = _layernorm(DEEPNORM_ALPHA * x_ref[...] + yacc_ref[...], g_ref[...], b_ref[...])


def _mlstm(x2, wh, wg, wgt, gb, cw, cb, hng, wout, g, b, *, B, T, L):
    D = x2.shape[1]
    NC = T // L
    H = B_HEADS
    assert T % L == 0 and L % LANES == 0

    def full(a):
        return pl.BlockSpec(a.shape, lambda bb, c, h: (0,) * a.ndim, pipeline_mode=pl.Buffered(1))

    xrow = pl.BlockSpec((L, D), lambda bb, c, h: (bb * NC + c, 0))
    return pl.pallas_call(
        functools.partial(_mlstm_body, L=L),
        grid=(B, NC, H),
        in_specs=[xrow] + [full(a) for a in (wh, wg, wgt, gb, cw, cb, hng, wout, g, b)],
        out_specs=xrow,
        out_shape=jax.ShapeDtypeStruct(x2.shape, _F32),
        scratch_shapes=[
            pltpu.VMEM((L, 3 * D), _MXU),
            pltpu.VMEM((H, B_QK_DIM, B_V_DIM + LANES), _F32),
            pltpu.VMEM((H, SUBLANES, LANES), _F32),
            pltpu.VMEM((H, SUBLANES, 2 * B_QK_DIM), _F32),
            pltpu.VMEM((L, D), _F32),
        ],
        compiler_params=pltpu.CompilerParams(
            dimension_semantics=("arbitrary", "arbitrary", "arbitrary"),
            vmem_limit_bytes=VMEM_LIMIT),
        name="mlstm_layer",
    )(x2, wh, wg, wgt, gb, cw, cb, hng, wout, g, b)


def _hi_lo(w):
    hi = w.astype(_MXU)
    return hi, (w - hi.astype(_F32)).astype(_MXU)


def kernel(x, a_w_in, a_kv_norm_g, a_w_uk, a_w_uv, a_w_out, a_ln_g, a_ln_b, b_w_in, b_i_bias,
           b_f_bias, b_conv_w, b_conv_b, b_head_norm_g, b_w_out, b_ln_g, b_ln_b):
    B, T, D = x.shape
    x2 = x.reshape(B * T, D)

    aw = a_w_in[0]
    n_q = A_HEADS * A_HEAD_DIM
    n_i = IDX_HEADS * IDX_DIM + IDX_DIM + IDX_HEADS
    o_c, o_i, o_z = n_q, n_q + A_KV_LATENT, n_q + A_KV_LATENT + n_i
    wq = aw[:, :o_c].astype(_MXU)
    wc = aw[:, o_c:o_i].astype(_MXU)
    wi = jnp.pad(aw[:, o_i:o_z], ((0, 0), (0, IDX_COLS - n_i)))
    wz = aw[:, o_z:].astype(_MXU)
    wih, wil = _hi_lo(wi)
    q, z, ckv, idx = _proj_a(x2, wq, wz, wc, wih, wil, a_kv_norm_g[0][None, :], tm=512)
    x1 = _attention(q, idx, ckv, z, x2, a_w_uk[0].astype(_MXU), a_w_uv[0].astype(_MXU),
                    a_w_out[0].astype(_MXU), a_ln_g[0][None, :], a_ln_b[0][None, :],
                    B=B, T=T, TQ=256)

    bw = b_w_in[0]
    H, DK, DV = B_HEADS, B_QK_DIM, B_V_DIM
    o_k, o_v, o_g, o_o, o_zz = H * DK, 2 * H * DK, 2 * H * DK + H * DV, 2 * H * DK + H * DV + 2 * H, \
        2 * H * DK + 2 * H * DV + 2 * H
    per_head = lambda o, n: bw[:, o:o + H * n].reshape(D, H, n)
    wh = jnp.concatenate([per_head(0, DK), per_head(o_k, DK), per_head(o_v, DV),
                          per_head(o_o, DV), per_head(o_zz, DV)], axis=2)
    wh = wh.transpose(1, 0, 2).astype(_MXU)
    wgate = jnp.stack([bw[:, o_g:o_g + H], bw[:, o_g + H:o_g + 2 * H]], axis=2)
    wgate = wgate.transpose(1, 0, 2)
    ghi, glo = _hi_lo(wgate)
    wg3 = jnp.concatenate([ghi, ghi, glo], axis=1)
    wg = jnp.pad(wg3, ((0, 0), (0, 0), (0, LANES - 2)))
    wgt = jnp.pad(wg3.transpose(0, 2, 1), ((0, 0), (0, 2 * SUBLANES - 2), (0, 0)))
    gb = jnp.stack([b_i_bias[0], b_f_bias[0]], axis=1)
    gb = jnp.broadcast_to(jnp.pad(gb, ((0, 0), (0, SUBLANES - 2)))[:, :, None], (H, SUBLANES, LANES))
    cw = b_conv_w[0]
    cw = jnp.concatenate([cw[:, :H * DK].reshape(CONV_WIDTH, H, DK),
                          cw[:, H * DK:].reshape(CONV_WIDTH, H, DK)], axis=2).transpose(1, 0, 2)
    cw = jnp.pad(cw, ((0, 0), (0, SUBLANES - CONV_WIDTH), (0, 0)))
    cb = b_conv_b[0]
    cb = jnp.concatenate([cb[:H * DK].reshape(H, 1, DK), cb[H * DK:].reshape(H, 1, DK)], axis=2)
    hng = b_head_norm_g[0][:, None, :]
    wout = b_w_out[0].reshape(H, DV, D).astype(_MXU)
    x2o = _mlstm(x1, wh, wg, wgt, gb, cw, cb, hng, wout, b_ln_g[0][None, :], b_ln_b[0][None, :],
                 B=B, T=T, L=256)
    return x2o.reshape(B, T, D)
```
